```python
import math
import jax
import jax.numpy as jnp
from jax import lax
import numpy as np

D_MODEL = 1024
BATCH = 16
SEQ = 2048
DEPTH = 4

N_MIXERS = 2
N_ATTN_LAYERS = (DEPTH + N_MIXERS - 1) // N_MIXERS
N_SSM_LAYERS = DEPTH // N_MIXERS

HEAD_DIM = 64
N_Q_HEADS = D_MODEL // HEAD_DIM
N_KV_HEADS = 4
Q_PER_KV = N_Q_HEADS // N_KV_HEADS
WINDOW = 128
ATTN_BLOCK = WINDOW
Q_DIM = N_Q_HEADS * HEAD_DIM
KV_DIM = N_KV_HEADS * HEAD_DIM
QKV_DIM = Q_DIM + 2 * KV_DIM

SSM_GROUP = 16
N_SSM_GROUPS = D_MODEL // SSM_GROUP
SSM_STATE = 64
DT_MIN = 1e-3
DT_MAX = 1e-1

N_EXPERT_GROUPS = 4
EXPERTS_PER_GROUP = 8
N_EXPERTS = N_EXPERT_GROUPS * EXPERTS_PER_GROUP
TOP_K = 2
D_EXPERT = 256
MOE_BLOCK = 128

RMS_EPS = 1e-6
N_ADA = 6

kernel_name = 'hybrid_swa_sink_s5_hmoe'


def rms_norm(x, gain):
    xf = x.astype(jnp.float32)
    xf = xf * lax.rsqrt(jnp.mean(xf * xf, axis=-1, keepdims=True) + RMS_EPS)
    return (xf * gain.astype(jnp.float32)).astype(x.dtype)


def sliding_window_attention(h, w_qkv, q_gain, k_gain, sinks, w_o):
    B, S, D = h.shape
    nb = S // ATTN_BLOCK
    qkv = h @ w_qkv
    q = qkv[..., :Q_DIM].reshape(B, S, N_Q_HEADS, HEAD_DIM)
    k = qkv[..., Q_DIM:Q_DIM + KV_DIM].reshape(B, S, N_KV_HEADS, HEAD_DIM)
    v = qkv[..., Q_DIM + KV_DIM:].reshape(B, S, N_KV_HEADS, HEAD_DIM)
    q = rms_norm(q, q_gain) * (HEAD_DIM ** -0.5)
    k = rms_norm(k, k_gain)
    q = q.reshape(B, nb, ATTN_BLOCK, N_KV_HEADS, Q_PER_KV, HEAD_DIM)

    def with_prev(t):
        t = t.reshape(B, nb, ATTN_BLOCK, N_KV_HEADS, HEAD_DIM)
        prev = jnp.pad(t, ((0, 0), (1, 0), (0, 0), (0, 0), (0, 0)))[:, :-1]
        return jnp.concatenate([prev, t], axis=2)

    kb, vb = with_prev(k), with_prev(v)
    scores = jnp.einsum('bnqhgd,bnshd->bnhgqs', q, kb).astype(jnp.float32)
    qi = jnp.arange(ATTN_BLOCK)[:, None]
    sj = jnp.arange(2 * ATTN_BLOCK)[None, :]
    rel = ATTN_BLOCK + qi - sj
    band = (rel >= 0) & (rel < WINDOW)
    has_prev = (jnp.arange(nb) > 0)[:, None, None]
    valid = band[None] & (has_prev | (sj >= ATTN_BLOCK)[None])
    scores = jnp.where(valid[None, :, None, None], scores, -jnp.inf)
    sink = jnp.broadcast_to(
        sinks.astype(jnp.float32).reshape(1, 1, N_KV_HEADS, Q_PER_KV, 1, 1),
        scores.shape[:-1] + (1,))
    probs = jax.nn.softmax(jnp.concatenate([scores, sink], axis=-1), axis=-1)[..., :-1]
    out = jnp.einsum('bnhgqs,bnshd->bnqhgd', probs.astype(vb.dtype), vb)
    return out.reshape(B, S, Q_DIM) @ w_o


def s5_mixer(h, w_in, lam_re, lam_im, log_dt, b_re, b_im, c_re, c_im, d_skip, w_glu):
    B, S, D = h.shape
    f32 = jnp.float32
    u = (h @ w_in).astype(f32)
    ug = u.reshape(B, S, N_SSM_GROUPS, SSM_GROUP)
    dt = jnp.exp(log_dt.astype(f32))[:, None]
    lr, li = lam_re.astype(f32), lam_im.astype(f32)
    mag = jnp.exp(lr * dt)
    ab_re, ab_im = mag * jnp.cos(li * dt), mag * jnp.sin(li * dt)
    den = lr * lr + li * li
    n_re = ab_re - 1.0
    f_re = (n_re * lr + ab_im * li) / den
    f_im = (ab_im * lr - n_re * li) / den
    br, bi = b_re.astype(f32), b_im.astype(f32)
    bb_re = f_re[..., None] * br - f_im[..., None] * bi
    bb_im = f_re[..., None] * bi + f_im[..., None] * br
    bu_re = jnp.einsum('bsgc,gpc->sbgp', ug, bb_re)
    bu_im = jnp.einsum('bsgc,gpc->sbgp', ug, bb_im)
    a_re = jnp.broadcast_to(ab_re[None, None], (S, 1, N_SSM_GROUPS, SSM_STATE))
    a_im = jnp.broadcast_to(ab_im[None, None], (S, 1, N_SSM_GROUPS, SSM_STATE))

    def combine(left, right):
        al_re, al_im, xl_re, xl_im = left
        ar_re, ar_im, xr_re, xr_im = right
        return (ar_re * al_re - ar_im * al_im,
                ar_re * al_im + ar_im * al_re,
                ar_re * xl_re - ar_im * xl_im + xr_re,
                ar_re * xl_im + ar_im * xl_re + xr_im)

    _, _, xs_re, xs_im = lax.associative_scan(combine, (a_re, a_im, bu_re, bu_im), axis=0)
    y = (jnp.einsum('sbgp,gcp->bsgc', xs_re, c_re.astype(f32))
         - jnp.einsum('sbgp,gcp->bsgc', xs_im, c_im.astype(f32)))
    y = y.reshape(B, S, D) + d_skip.astype(f32) * u
    y = jax.nn.gelu(y).astype(h.dtype)
    z = y @ w_glu
    return z[..., :D] * jax.nn.sigmoid(z[..., D:])


def hierarchical_moe(h, w_group, b_group, w_expert, b_expert, w_gate, w_up, w_down):
    B, S, D = h.shape
    N = B * S
    xt = h.reshape(N, D)
    g_prob = jax.nn.softmax((xt @ w_group).astype(jnp.float32) + b_group.astype(jnp.float32), axis=-1)
    g_w, g_idx = lax.top_k(g_prob, 1)
    e_logits = ((xt @ w_expert).astype(jnp.float32) + b_expert.astype(jnp.float32)
                ).reshape(N, N_EXPERT_GROUPS, EXPERTS_PER_GROUP)
    e_sel = jnp.take_along_axis(e_logits, g_idx[:, :, None], axis=1)[:, 0]
    e_w, e_idx = lax.top_k(jax.nn.softmax(e_sel, axis=-1), TOP_K)
    gates = g_w * (e_w / jnp.sum(e_w, axis=-1, keepdims=True))
    experts = g_idx * EXPERTS_PER_GROUP + e_idx

    A = N * TOP_K
    flat_e = experts.reshape(A).astype(jnp.int32)
    flat_tok = jnp.repeat(jnp.arange(N, dtype=jnp.int32), TOP_K)
    flat_gate = gates.reshape(A)
    order = jnp.argsort(flat_e)
    s_e, s_tok, s_gate = flat_e[order], flat_tok[order], flat_gate[order]
    counts = jnp.zeros((N_EXPERTS,), jnp.int32).at[flat_e].add(1)
    starts = jnp.cumsum(counts) - counts
    padded = (counts + MOE_BLOCK - 1) // MOE_BLOCK * MOE_BLOCK
    pad_ends = jnp.cumsum(padded)
    pad_starts = pad_ends - padded
    dest = pad_starts[s_e] + jnp.arange(A, dtype=jnp.int32) - starts[s_e]
    n_blocks = -(-A // MOE_BLOCK) + N_EXPERTS
    P = n_blocks * MOE_BLOCK
    row_tok = jnp.zeros((P,), jnp.int32).at[dest].set(s_tok)
    row_gate = jnp.zeros((P,), jnp.float32).at[dest].set(s_gate)
    block_start = jnp.arange(n_blocks, dtype=jnp.int32) * MOE_BLOCK
    block_expert = jnp.minimum(jnp.searchsorted(pad_ends, block_start, side='right'),
                               N_EXPERTS - 1)
    xb = xt[row_tok].reshape(n_blocks, MOE_BLOCK, D)

    def expert_block(args):
        xblk, e = args
        hid = jax.nn.silu(xblk @ w_gate[e]) * (xblk @ w_up[e])
        return hid @ w_down[e]

    yb = lax.map(expert_block, (xb, block_expert))
    y_rows = yb.reshape(P, D) * row_gate[:, None].astype(yb.dtype)
    y = jax.ops.segment_sum(y_rows, row_tok, num_segments=N)
    return y.reshape(B, S, D)


def setup_inputs(seed: int = 0) -> dict:
    key = jax.random.key(seed)
    ks = iter(jax.random.split(key, 40))

    def nrm(shape, scale):
        return scale * jax.random.normal(next(ks), shape, jnp.float32)

    L, LA, LS = DEPTH, N_ATTN_LAYERS, N_SSM_LAYERS
    G, P, C = N_SSM_GROUPS, SSM_STATE, SSM_GROUP
    E, F, D = N_EXPERTS, D_EXPERT, D_MODEL
    lam_im = jnp.broadcast_to(math.pi * jnp.arange(P, dtype=jnp.float32), (LS, G, P))
    return {
        'x': nrm((BATCH, SEQ, D), 1.0),
        'c': nrm((BATCH, D), 1.0),
        'norm_mix': 1.0 + nrm((L, D), 0.02),
        'norm_ffn': 1.0 + nrm((L, D), 0.02),
        'w_ada': nrm((L, D, N_ADA * D), 0.5 * D ** -0.5),
        'b_ada': nrm((L, N_ADA * D), 0.02),
        'attn_w_qkv': nrm((LA, D, QKV_DIM), D ** -0.5),
        'attn_q_gain': 1.0 + nrm((LA, HEAD_DIM), 0.02),
        'attn_k_gain': 1.0 + nrm((LA, HEAD_DIM), 0.02),
        'attn_sinks': nrm((LA, N_Q_HEADS), 0.5),
        'attn_w_o': nrm((LA, Q_DIM, D), Q_DIM ** -0.5),
        'ssm_w_in': nrm((LS, D, D), D ** -0.5),
        'ssm_lam_re': -0.5 + nrm((LS, G, P), 0.01),
        'ssm_lam_im': lam_im,
        'ssm_log_dt': jax.random.uniform(next(ks), (LS, G), jnp.float32,
                                         math.log(DT_MIN), math.log(DT_MAX)),
        'ssm_b_re': nrm((LS, G, P, C), (2.0 * C) ** -0.5),
        'ssm_b_im': nrm((LS, G, P, C), (2.0 * C) ** -0.5),
        'ssm_c_re': nrm((LS, G, C, P), (2.0 * P) ** -0.5),
        'ssm_c_im': nrm((LS, G, C, P), (2.0 * P) ** -0.5),
        'ssm_d': nrm((LS, D), 1.0),
        'ssm_w_glu': nrm((LS, D, 2 * D), D ** -0.5),
        'moe_w_group': nrm((L, D, N_EXPERT_GROUPS), D ** -0.5),
        'moe_b_group': nrm((L, N_EXPERT_GROUPS), 0.01),
        'moe_w_expert': nrm((L, D, E), D ** -0.5),
        'moe_b_expert': nrm((L, E), 0.01),
        'moe_w_gate': nrm((L, E, D, F), D ** -0.5),
        'moe_w_up': nrm((L, E, D, F), D ** -0.5),
        'moe_w_down': nrm((L, E, F, D), F ** -0.5),
    }


def reference(x, c, norm_mix, norm_ffn, w_ada, b_ada,
              attn_w_qkv, attn_q_gain, attn_k_gain, attn_sinks, attn_w_o,
              ssm_w_in, ssm_lam_re, ssm_lam_im, ssm_log_dt, ssm_b_re, ssm_b_im,
              ssm_c_re, ssm_c_im, ssm_d, ssm_w_glu,
              moe_w_group, moe_b_group, moe_w_expert, moe_b_expert,
              moe_w_gate, moe_w_up, moe_w_down):
    c_act = jax.nn.silu(c)
    for layer in range(DEPTH):
        ada = (c_act @ w_ada[layer] + b_ada[layer])[:, None, :]
        sh1, sc1, g1, sh2, sc2, g2 = jnp.split(ada, N_ADA, axis=-1)
        h = rms_norm(x, norm_mix[layer]) * (1.0 + sc1) + sh1
        if layer % N_MIXERS == 0:
            i = layer // N_MIXERS
            mix = sliding_window_attention(h, attn_w_qkv[i], attn_q_gain[i], attn_k_gain[i],
                                           attn_sinks[i], attn_w_o[i])
        else:
            i = layer // N_MIXERS
            mix = s5_mixer(h, ssm_w_in[i], ssm_lam_re[i], ssm_lam_im[i], ssm_log_dt[i],
                           ssm_b_re[i], ssm_b_im[i], ssm_c_re[i], ssm_c_im[i],
                           ssm_d[i], ssm_w_glu[i])
        x = x + g1 * mix
        h = rms_norm(x, norm_ffn[layer]) * (1.0 + sc2) + sh2
        x = x + g2 * hierarchical_moe(h, moe_w_group[layer], moe_b_group[layer],
                                      moe_w_expert[layer], moe_b_expert[layer],
                                      moe_w_gate[layer], moe_w_up[layer], moe_w_down[layer])
    return x
```

```python
import functools
import math

import jax
import jax.numpy as jnp
from jax import lax
from jax.experimental import pallas as pl
from jax.experimental.pallas import tpu as pltpu

F32 = jnp.float32
BF16 = jnp.bfloat16

RMS_EPS = 1e-6
HEAD_DIM = 64
N_KV_HEADS = 4
Q_PER_KV = 4
ATTN_BLOCK = 128
SSM_GROUP = 16
SSM_STATE = 64
N_EXPERT_GROUPS = 4
EXPERTS_PER_GROUP = 8
N_EXPERTS = N_EXPERT_GROUPS * EXPERTS_PER_GROUP
N_ADA = 6

LANES = 128
SSM_CHUNK_GROUPS = LANES // SSM_GROUP
MOE_ROWS = 256
ROUTE_W = 8
VMEM_LIMIT = 48 * 1024 * 1024


def _cparams(sem):
    return pltpu.CompilerParams(dimension_semantics=sem, vmem_limit_bytes=VMEM_LIMIT)


def _norm_mod(x, nw, sc, sh):
    ms = jnp.mean(x * x, axis=-1, keepdims=True)
    return x * lax.rsqrt(ms + RMS_EPS) * (nw * (1.0 + sc)) + sh


def _ada_kernel(c_ref, w_ref, b_ref, o_ref):
    c = c_ref[...]
    ca = c * jax.nn.sigmoid(c)
    o_ref[0] = jnp.dot(ca.astype(BF16), w_ref[0].astype(BF16),
                       preferred_element_type=F32) + b_ref[0]


def _ada(c, w_ada, b_ada):
    L, D, W = w_ada.shape
    B = c.shape[0]
    tn = W // 4
    return pl.pallas_call(
        _ada_kernel,
        grid=(L, W // tn),
        in_specs=[pl.BlockSpec((B, D), lambda l, j: (0, 0)),
                  pl.BlockSpec((1, D, tn), lambda l, j: (l, 0, j)),
                  pl.BlockSpec((1, 1, tn), lambda l, j: (l, 0, j))],
        out_specs=pl.BlockSpec((1, B, tn), lambda l, j: (l, 0, j)),
        out_shape=jax.ShapeDtypeStruct((L, B, W), F32),
        compiler_params=_cparams(("arbitrary", "arbitrary")),
        name="ada",
    )(c, w_ada, b_ada.reshape(L, 1, W))


def _qkv_kernel(x_ref, ada_ref, nw_ref, w_ref, o_ref):
    a = ada_ref[0]
    h = _norm_mod(x_ref[0], nw_ref[...], a[1:2], a[0:1])
    o_ref[0] = jnp.dot(h.astype(BF16), w_ref[...], preferred_element_type=F32).astype(BF16)


def _qkv(x, ada, nw, w, tm):
    B, S, D = x.shape
    W = w.shape[1]
    return pl.pallas_call(
        _qkv_kernel,
        grid=(B, S // tm),
        in_specs=[pl.BlockSpec((1, tm, D), lambda b, i: (b, i, 0)),
                  pl.BlockSpec((1, N_ADA, D), lambda b, i: (b, 0, 0)),
                  pl.BlockSpec((1, D), lambda b, i: (0, 0)),
                  pl.BlockSpec((D, W), lambda b, i: (0, 0))],
        out_specs=pl.BlockSpec((1, tm, W), lambda b, i: (b, i, 0)),
        out_shape=jax.ShapeDtypeStruct((B, S, W), BF16),
        compiler_params=_cparams(("arbitrary", "arbitrary")),
        name="qkv",
    )(x, ada, nw, w)


def _head_rms(t, gain):
    ms = jnp.mean(t * t, axis=-1, keepdims=True)
    return t * lax.rsqrt(ms + RMS_EPS) * gain


def _attn_kernel(q_ref, kvc_ref, kvp_ref, qg_ref, kg_ref, sink_ref, o_ref):
    n = pl.program_id(1)
    blk = ATTN_BLOCK
    hd = HEAD_DIM
    kv_dim = N_KV_HEADS * hd
    kvc = kvc_ref[0]
    kvp = kvp_ref[0]
    qg = qg_ref[...] * (hd ** -0.5)
    kg = kg_ref[...]
    rows = Q_PER_KV * blk
    qi = lax.broadcasted_iota(jnp.int32, (rows, 2 * blk), 0) & (blk - 1)
    sj = lax.broadcasted_iota(jnp.int32, (rows, 2 * blk), 1)
    rel = blk + qi - sj
    first_key = jnp.where(n > 0, 0, blk)
    valid = (rel >= 0) & (rel < blk) & (sj >= first_key)
    outs = []
    for h in range(N_KV_HEADS):
        k = jnp.concatenate([kvp[:, h * hd:(h + 1) * hd], kvc[:, h * hd:(h + 1) * hd]], axis=0)
        k = _head_rms(k.astype(F32), kg).astype(BF16)
        v = jnp.concatenate([kvp[:, kv_dim + h * hd:kv_dim + (h + 1) * hd],
                             kvc[:, kv_dim + h * hd:kv_dim + (h + 1) * hd]], axis=0)
        q4 = jnp.concatenate(
            [q_ref[0, :, (h * Q_PER_KV + g) * hd:(h * Q_PER_KV + g + 1) * hd]
             for g in range(Q_PER_KV)], axis=0)
        q4 = _head_rms(q4.astype(F32), qg).astype(BF16)
        s = lax.dot_general(q4, k, (((1,), (1,)), ((), ())), preferred_element_type=F32)
        s = jnp.where(valid, s, -jnp.inf)
        sink = jnp.concatenate(
            [jnp.broadcast_to(sink_ref[h * Q_PER_KV + g], (blk, 1)) for g in range(Q_PER_KV)],
            axis=0)
        m = jnp.maximum(jnp.max(s, axis=-1, keepdims=True), sink)
        p = jnp.exp(s - m)
        denom = jnp.sum(p, axis=-1, keepdims=True) + jnp.exp(sink - m)
        o4 = jnp.dot(p.astype(BF16), v, preferred_element_type=F32) / denom
        outs.append(jnp.concatenate([o4[g * blk:(g + 1) * blk] for g in range(Q_PER_KV)], axis=1))
    o_ref[0] = jnp.concatenate(outs, axis=1).astype(BF16)


def _attn(qkv, q_gain, k_gain, sinks):
    B, S, W = qkv.shape
    q_dim = N_KV_HEADS * Q_PER_KV * HEAD_DIM
    kv2 = 2 * N_KV_HEADS * HEAD_DIM
    assert W == q_dim + kv2 and q_dim % kv2 == 0
    kv_blk = q_dim // kv2
    nb = S // ATTN_BLOCK
    nh = N_KV_HEADS * Q_PER_KV
    return pl.pallas_call(
        _attn_kernel,
        grid=(B, nb),
        in_specs=[pl.BlockSpec((1, ATTN_BLOCK, q_dim), lambda b, n: (b, n, 0)),
                  pl.BlockSpec((1, ATTN_BLOCK, kv2), lambda b, n: (b, n, kv_blk)),
                  pl.BlockSpec((1, ATTN_BLOCK, kv2),
                               lambda b, n: (b, jnp.maximum(n - 1, 0), kv_blk)),
                  pl.BlockSpec((1, HEAD_DIM), lambda b, n: (0, 0)),
                  pl.BlockSpec((1, HEAD_DIM), lambda b, n: (0, 0)),
                  pl.BlockSpec((nh, 1, 1), lambda b, n: (0, 0, 0))],
        out_specs=pl.BlockSpec((1, ATTN_BLOCK, q_dim), lambda b, n: (b, n, 0)),
        out_shape=jax.ShapeDtypeStruct((B, S, q_dim), BF16),
        compiler_params=_cparams(("arbitrary", "arbitrary")),
        name="attn",
    )(qkv, qkv, qkv, q_gain.reshape(1, HEAD_DIM), k_gain.reshape(1, HEAD_DIM),
      sinks.reshape(nh, 1, 1))


def _proj_res_kernel(a_ref, x_ref, ada_ref, w_ref, o_ref, *, gate_row):
    g = ada_ref[0][gate_row:gate_row + 1]
    o_ref[0] = x_ref[0] + g * jnp.dot(a_ref[0], w_ref[...], preferred_element_type=F32)


def _proj_res(a, x, ada, w, tm, gate_row):
    B, S, D = x.shape
    K = a.shape[-1]
    return pl.pallas_call(
        functools.partial(_proj_res_kernel, gate_row=gate_row),
        grid=(B, S // tm),
        in_specs=[pl.BlockSpec((1, tm, K), lambda b, i: (b, i, 0)),
                  pl.BlockSpec((1, tm, D), lambda b, i: (b, i, 0)),
                  pl.BlockSpec((1, N_ADA, D), lambda b, i: (b, 0, 0)),
                  pl.BlockSpec((K, D), lambda b, i: (0, 0))],
        out_specs=pl.BlockSpec((1, tm, D), lambda b, i: (b, i, 0)),
        out_shape=jax.ShapeDtypeStruct((B, S, D), F32),
        compiler_params=_cparams(("arbitrary", "arbitrary")),
        name="proj_res",
    )(a, x, ada, w)


def _ssm_in_kernel(x_ref, ada_ref, nw_ref, w_ref, o_ref):
    a = ada_ref[0]
    h = _norm_mod(x_ref[0], nw_ref[...], a[1:2], a[0:1])
    o_ref[...] = jnp.dot(h.astype(BF16), w_ref[...], preferred_element_type=F32)


def _ssm_in(x, ada, nw, w, tm):
    B, S, D = x.shape
    return pl.pallas_call(
        _ssm_in_kernel,
        grid=(B, S // tm),
        in_specs=[pl.BlockSpec((1, tm, D), lambda b, i: (b, i, 0)),
                  pl.BlockSpec((1, N_ADA, D), lambda b, i: (b, 0, 0)),
                  pl.BlockSpec((1, D), lambda b, i: (0, 0)),
                  pl.BlockSpec((D, D), lambda b, i: (0, 0))],
        out_specs=pl.BlockSpec((tm, D), lambda b, i: (i, b)),
        out_shape=jax.ShapeDtypeStruct((S, B * D), F32),
        compiler_params=_cparams(("arbitrary", "arbitrary")),
        name="ssm_in",
    )(x, ada, nw, w)


def _ssm_kernel(u_ref, b_ref, c_ref, ar_ref, ai_ref, d_ref, y_ref, xs_ref, st_ref, *, nbatch, ts):
    half = xs_ref.shape[1] // 2

    @pl.when(pl.program_id(1) == 0)
    def _():
        st_ref[...] = jnp.zeros_like(st_ref)

    u = u_ref[...]
    xs_ref[...] = jnp.dot(u.astype(BF16), b_ref[0], preferred_element_type=F32)
    ar = ar_ref[0]
    ai = ai_ref[0]

    def step(s, carry):
        xr, xi = carry
        r0 = pl.multiple_of(s * nbatch, nbatch)
        br = xs_ref[pl.ds(r0, nbatch), 0:half]
        bi = xs_ref[pl.ds(r0, nbatch), half:2 * half]
        nr = ar * xr - ai * xi + br
        ni = ar * xi + ai * xr + bi
        xs_ref[pl.ds(r0, nbatch), 0:half] = nr
        xs_ref[pl.ds(r0, nbatch), half:2 * half] = ni
        return nr, ni

    xr, xi = lax.fori_loop(0, ts, step, (st_ref[:, 0:half], st_ref[:, half:2 * half]), unroll=4)
    st_ref[:, 0:half] = xr
    st_ref[:, half:2 * half] = xi
    y_ref[...] = (jnp.dot(xs_ref[...].astype(BF16), c_ref[0], preferred_element_type=F32)
                  + d_ref[...] * u)


def _ssm_core(u_tm, bmat, cmat, ar, ai, dskip, nbatch, ts):
    R, D = u_tm.shape
    nchunk = D // LANES
    rows = ts * nbatch
    nstate = bmat.shape[2]
    return pl.pallas_call(
        functools.partial(_ssm_kernel, nbatch=nbatch, ts=ts),
        grid=(nchunk, R // rows),
        in_specs=[pl.BlockSpec((rows, LANES), lambda j, i: (i, j)),
                  pl.BlockSpec((1, LANES, nstate), lambda j, i: (j, 0, 0)),
                  pl.BlockSpec((1, nstate, LANES), lambda j, i: (j, 0, 0)),
                  pl.BlockSpec((1, 1, nstate // 2), lambda j, i: (j, 0, 0)),
                  pl.BlockSpec((1, 1, nstate // 2), lambda j, i: (j, 0, 0)),
                  pl.BlockSpec((1, LANES), lambda j, i: (0, j))],
        out_specs=pl.BlockSpec((rows, LANES), lambda j, i: (i, j)),
        out_shape=jax.ShapeDtypeStruct((R, D), F32),
        scratch_shapes=[pltpu.VMEM((rows, nstate), F32), pltpu.VMEM((nbatch, nstate), F32)],
        compiler_params=_cparams(("arbitrary", "arbitrary")),
        name="ssm_core",
    )(u_tm, bmat, cmat, ar, ai, dskip)


def _gelu_tanh(y):
    return 0.5 * y * (1.0 + jnp.tanh(math.sqrt(2.0 / math.pi) * (y + 0.044715 * (y * y * y))))


def _glu_kernel(y_ref, x_ref, ada_ref, w_ref, o_ref):
    D = x_ref.shape[-1]
    z = jnp.dot(_gelu_tanh(y_ref[...]).astype(BF16), w_ref[...], preferred_element_type=F32)
    mix = z[:, :D] * jax.nn.sigmoid(z[:, D:])
    o_ref[0] = x_ref[0] + ada_ref[0][2:3] * mix


def _glu_res(y_tm, x, ada, w, tm):
    B, S, D = x.shape
    return pl.pallas_call(
        _glu_kernel,
        grid=(B, S // tm),
        in_specs=[pl.BlockSpec((tm, D), lambda b, i: (i, b)),
                  pl.BlockSpec((1, tm, D), lambda b, i: (b, i, 0)),
                  pl.BlockSpec((1, N_ADA, D), lambda b, i: (b, 0, 0)),
                  pl.BlockSpec((D, 2 * D), lambda b, i: (0, 0))],
        out_specs=pl.BlockSpec((1, tm, D), lambda b, i: (b, i, 0)),
        out_shape=jax.ShapeDtypeStruct((B, S, D), F32),
        compiler_params=_cparams(("arbitrary", "arbitrary")),
        name="glu_res",
    )(y_tm, x, ada, w)


def _ssm_params(lam_re, lam_im, log_dt, b_re, b_im, c_re, c_im):
    G, P = lam_re.shape
    C = b_re.shape[2]
    dt = jnp.exp(log_dt.astype(F32))[:, None]
    lr, li = lam_re.astype(F32), lam_im.astype(F32)
    mag = jnp.exp(lr * dt)
    ab_re, ab_im = mag * jnp.cos(li * dt), mag * jnp.sin(li * dt)
    den = lr * lr + li * li
    n_re = ab_re - 1.0
    f_re = (n_re * lr + ab_im * li) / den
    f_im = (ab_im * lr - n_re * li) / den
    br, bi = b_re.astype(F32), b_im.astype(F32)
    bb_re = f_re[..., None] * br - f_im[..., None] * bi
    bb_im = f_re[..., None] * bi + f_im[..., None] * br
    gc = SSM_CHUNK_GROUPS
    nchunk = G // gc
    eye = jnp.eye(gc, dtype=F32)

    def blockdiag_in(m):
        m = m.reshape(nchunk, gc, P, C)
        return jnp.einsum('ngpc,gh->ngchp', m, eye).reshape(nchunk, gc * C, gc * P)

    def blockdiag_out(m):
        m = m.reshape(nchunk, gc, C, P)
        return jnp.einsum('ngcp,gh->ngphc', m, eye).reshape(nchunk, gc * P, gc * C)

    bmat = jnp.concatenate([blockdiag_in(bb_re), blockdiag_in(bb_im)], axis=2).astype(BF16)
    cmat = jnp.concatenate([blockdiag_out(c_re.astype(F32)),
                            -blockdiag_out(c_im.astype(F32))], axis=1).astype(BF16)
    ar = ab_re.reshape(nchunk, 1, gc * P)
    ai = ab_im.reshape(nchunk, 1, gc * P)
    return bmat, cmat, ar, ai


def _router_kernel(x_ref, ada_ref, nw_ref, w_ref, b_ref, tri_ref, h_ref, r_ref, cnt_ref, run_ref):
    first = (pl.program_id(0) == 0) & (pl.program_id(1) == 0)

    @pl.when(first)
    def _():
        run_ref[...] = jnp.zeros_like(run_ref)

    a = ada_ref[0]
    h = _norm_mod(x_ref[0], nw_ref[...], a[4:5], a[3:4])
    h_hi = h.astype(BF16)
    h_ref[0] = h_hi
    h_lo = (h - h_hi.astype(F32)).astype(BF16)
    w = w_ref[...]
    t = jnp.dot(h_hi, w, preferred_element_type=F32)
    logits = (t[:, :LANES] + t[:, LANES:]
              + jnp.dot(h_lo, w[:, :LANES], preferred_element_type=F32)) + b_ref[...]
    tm = logits.shape[0]
    lane = lax.broadcasted_iota(jnp.int32, (tm, LANES), 1)
    lane_f = lane.astype(F32)
    ng, epg, ne = N_EXPERT_GROUPS, EXPERTS_PER_GROUP, N_EXPERTS
    neg = -jnp.inf
    is_g = (lane >= ne) & (lane < ne + ng)
    lg = jnp.where(is_g, logits, neg)
    mg = jnp.max(lg, axis=-1, keepdims=True)
    g_w = 1.0 / jnp.sum(jnp.exp(lg - mg), axis=-1, keepdims=True)
    g_idx = jnp.min(jnp.where(lg == mg, lane_f, float(LANES)), axis=-1,
                    keepdims=True).astype(jnp.int32) - ne
    in_grp = (lane >= g_idx * epg) & (lane < (g_idx + 1) * epg)
    le = jnp.where(in_grp, logits, neg)
    me = jnp.max(le, axis=-1, keepdims=True)
    pe = jnp.exp(le - me)
    pe = pe / jnp.sum(pe, axis=-1, keepdims=True)
    p1 = jnp.max(pe, axis=-1, keepdims=True)
    e1 = jnp.min(jnp.where((pe == p1) & in_grp, lane_f, float(LANES)), axis=-1,
                 keepdims=True).astype(jnp.int32)
    pe2 = jnp.where(lane == e1, -1.0, jnp.where(in_grp, pe, -1.0))
    p2 = jnp.max(pe2, axis=-1, keepdims=True)
    e2 = jnp.min(jnp.where(pe2 == p2, lane_f, float(LANES)), axis=-1,
                 keepdims=True).astype(jnp.int32)
    gate1 = g_w * (p1 / (p1 + p2))
    gate2 = g_w * (p2 / (p1 + p2))
    oh1 = lane == e1
    oh2 = lane == e2
    oh = jnp.where(oh1 | oh2, 1.0, 0.0)
    before = jnp.dot(tri_ref[...], oh.astype(BF16), preferred_element_type=F32) + run_ref[...]
    rank1 = jnp.sum(jnp.where(oh1, before, 0.0), axis=-1, keepdims=True)
    rank2 = jnp.sum(jnp.where(oh2, before, 0.0), axis=-1, keepdims=True)
    run_ref[...] = run_ref[...] + jnp.sum(oh, axis=0, keepdims=True)
    cnt_ref[...] = run_ref[...].astype(jnp.int32)
    col = lax.broadcasted_iota(jnp.int32, (tm, ROUTE_W), 1)
    rec = jnp.where(col == 0, e1,
          jnp.where(col == 1, e2,
          jnp.where(col == 2, rank1.astype(jnp.int32),
          jnp.where(col == 3, rank2.astype(jnp.int32),
          jnp.where(col == 4, pltpu.bitcast(gate1, jnp.int32),
          jnp.where(col == 5, pltpu.bitcast(gate2, jnp.int32), 0))))))
    r_ref[0] = rec


def _router(x, ada, nw, w_group, b_group, w_expert, b_expert, tm):
    B, S, D = x.shape
    ne, ng = N_EXPERTS, N_EXPERT_GROUPS
    wr = jnp.zeros((D, LANES), F32).at[:, :ne].set(w_expert).at[:, ne:ne + ng].set(w_group)
    w_hi = wr.astype(BF16)
    w_lo = (wr - w_hi.astype(F32)).astype(BF16)
    w2 = jnp.concatenate([w_hi, w_lo], axis=1)
    br = jnp.zeros((1, LANES), F32).at[0, :ne].set(b_expert).at[0, ne:ne + ng].set(b_group)
    tri = (lax.broadcasted_iota(jnp.int32, (tm, tm), 0)
           > lax.broadcasted_iota(jnp.int32, (tm, tm), 1)).astype(BF16)
    return pl.pallas_call(
        _router_kernel,
        grid=(B, S // tm),
        in_specs=[pl.BlockSpec((1, tm, D), lambda b, i: (b, i, 0)),
                  pl.BlockSpec((1, N_ADA, D), lambda b, i: (b, 0, 0)),
                  pl.BlockSpec((1, D), lambda b, i: (0, 0)),
                  pl.BlockSpec((D, 2 * LANES), lambda b, i: (0, 0)),
                  pl.BlockSpec((1, LANES), lambda b, i: (0, 0)),
                  pl.BlockSpec((tm, tm), lambda b, i: (0, 0))],
        out_specs=[pl.BlockSpec((1, tm, D), lambda b, i: (b, i, 0)),
                   pl.BlockSpec((1, tm, ROUTE_W), lambda b, i: (b, i, 0)),
                   pl.BlockSpec((1, LANES), lambda b, i: (0, 0))],
        out_shape=[jax.ShapeDtypeStruct((B, S, D), BF16),
                   jax.ShapeDtypeStruct((B, S, ROUTE_W), jnp.int32),
                   jax.ShapeDtypeStruct((1, LANES), jnp.int32)],
        scratch_shapes=[pltpu.VMEM((1, LANES), F32)],
        compiler_params=_cparams(("arbitrary", "arbitrary")),
        name="router",
    )(x, ada, nw, w2, br, tri)


def _expert_kernel(be_ref, xb_ref, wg_ref, wu_ref, wd_ref, o_ref, wg_s, wu_s, wd_s):
    i = pl.program_id(0)
    changed = (i == 0) | (be_ref[i] != be_ref[jnp.maximum(i - 1, 0)])

    @pl.when(changed)
    def _():
        wg_s[...] = wg_ref[0].astype(BF16)
        wu_s[...] = wu_ref[0].astype(BF16)
        wd_s[...] = wd_ref[0].astype(BF16)

    xb = xb_ref[...]
    g = jnp.dot(xb, wg_s[...], preferred_element_type=F32)
    u = jnp.dot(xb, wu_s[...], preferred_element_type=F32)
    hid = (g * jax.nn.sigmoid(g)) * u
    o_ref[...] = jnp.dot(hid.astype(BF16), wd_s[...], preferred_element_type=F32)


def _experts(block_expert, xb, w_gate, w_up, w_down):
    P, D = xb.shape
    E, _, F = w_gate.shape
    nblk = P // MOE_ROWS
    grid_spec = pltpu.PrefetchScalarGridSpec(
        num_scalar_prefetch=1,
        grid=(nblk,),
        in_specs=[pl.BlockSpec((MOE_ROWS, D), lambda i, be: (i, 0)),
                  pl.BlockSpec((1, D, F), lambda i, be: (be[i], 0, 0)),
                  pl.BlockSpec((1, D, F), lambda i, be: (be[i], 0, 0)),
                  pl.BlockSpec((1, F, D), lambda i, be: (be[i], 0, 0))],
        out_specs=pl.BlockSpec((MOE_ROWS, D), lambda i, be: (i, 0)),
        scratch_shapes=[pltpu.VMEM((D, F), BF16), pltpu.VMEM((D, F), BF16), pltpu.VMEM((F, D), BF16)],
    )
    return pl.pallas_call(
        _expert_kernel,
        grid_spec=grid_spec,
        out_shape=jax.ShapeDtypeStruct((P, D), F32),
        compiler_params=_cparams(("arbitrary",)),
        name="experts",
    )(block_expert, xb, w_gate, w_up, w_down)


def _res_kernel(y_ref, x_ref, ada_ref, o_ref, *, gate_row):
    o_ref[0] = x_ref[0] + ada_ref[0][gate_row:gate_row + 1] * y_ref[0]


def _res(y, x, ada, tm, gate_row):
    B, S, D = x.shape
    return pl.pallas_call(
        functools.partial(_res_kernel, gate_row=gate_row),
        grid=(B, S // tm),
        in_specs=[pl.BlockSpec((1, tm, D), lambda b, i: (b, i, 0)),
                  pl.BlockSpec((1, tm, D), lambda b, i: (b, i, 0)),
                  pl.BlockSpec((1, N_ADA, D), lambda b, i: (b, 0, 0))],
        out_specs=pl.BlockSpec((1, tm, D), lambda b, i: (b, i, 0)),
        out_shape=jax.ShapeDtypeStruct((B, S, D), F32),
        compiler_params=_cparams(("arbitrary", "arbitrary")),
        name="res",
    )(y, x, ada)


def _moe(x, ada, nw, w_group, b_group, w_expert, b_expert, w_gate, w_up, w_down, tm):
    B, S, D = x.shape
    N = B * S
    E = N_EXPERTS
    h, route, counts = _router(x, ada, nw, w_group, b_group, w_expert, b_expert, tm)
    route = route.reshape(N, ROUTE_W)
    counts = counts[0, :E]
    padded = (counts + MOE_ROWS - 1) // MOE_ROWS * MOE_ROWS
    pad_ends = jnp.cumsum(padded)
    pad_starts = pad_ends - padded
    nblk = (N * 2) // MOE_ROWS + E
    P = nblk * MOE_ROWS
    e12 = route[:, 0:2]
    dest = pad_starts[e12] + route[:, 2:4]
    gates = lax.bitcast_convert_type(route[:, 4:6], F32)
    block_start = jnp.arange(nblk, dtype=jnp.int32) * MOE_ROWS
    block_expert = jnp.minimum(jnp.searchsorted(pad_ends, block_start, side='right'),
                               E - 1).astype(jnp.int32)
    tok = jnp.repeat(jnp.arange(N, dtype=jnp.int32), 2)
    row_tok = jnp.zeros((P,), jnp.int32).at[dest.reshape(-1)].set(tok)
    xb = h.reshape(N, D)[row_tok]
    yb = _experts(block_expert, xb, w_gate, w_up, w_down)
    y = gates[:, 0:1] * yb[dest[:, 0]] + gates[:, 1:2] * yb[dest[:, 1]]
    return _res(y.reshape(B, S, D), x, ada, tm, 5)


def kernel(x, c, norm_mix, norm_ffn, w_ada, b_ada, attn_w_qkv, attn_q_gain, attn_k_gain, attn_sinks, attn_w_o, ssm_w_in, ssm_lam_re, ssm_lam_im, ssm_log_dt, ssm_b_re, ssm_b_im, ssm_c_re, ssm_c_im, ssm_d, ssm_w_glu, moe_w_group, moe_b_group, moe_w_expert, moe_b_expert, moe_w_gate, moe_w_up, moe_w_down):
    B, S, D = x.shape
    L = w_ada.shape[0]
    tm = min(512, S)
    ts = min(64, S)
    ada_all = _ada(c, w_ada, b_ada).reshape(L, B, N_ADA, D)
    for layer in range(L):
        ada = ada_all[layer]
        nw = norm_mix[layer].reshape(1, D)
        i = layer // 2
        if layer % 2 == 0:
            qkv = _qkv(x, ada, nw, attn_w_qkv[i].astype(BF16), tm)
            att = _attn(qkv, attn_q_gain[i], attn_k_gain[i], attn_sinks[i])
            x = _proj_res(att, x, ada, attn_w_o[i].astype(BF16), tm, 2)
        else:
            u_tm = _ssm_in(x, ada, nw, ssm_w_in[i].astype(BF16), tm)
            bmat, cmat, ar, ai = _ssm_params(ssm_lam_re[i], ssm_lam_im[i], ssm_log_dt[i],
                                             ssm_b_re[i], ssm_b_im[i], ssm_c_re[i], ssm_c_im[i])
            y_tm = _ssm_core(u_tm.reshape(S * B, D), bmat, cmat, ar, ai,
                             ssm_d[i].reshape(1, D), B, ts)
            x = _glu_res(y_tm.reshape(S, B * D), x, ada, ssm_w_glu[i].astype(BF16), tm)
        x = _moe(x, ada, norm_ffn[layer].reshape(1, D), moe_w_group[layer], moe_b_group[layer],
                 moe_w_expert[layer], moe_b_expert[layer], moe_w_gate[layer], moe_w_up[layer],
                 moe_w_down[layer], tm)
    return x
```
